```python
import math
import jax, jax.numpy as jnp
from jax import lax
import numpy as np

D_MODEL = 4096
BATCH = 4
SEQ = 2048
DEPTH = 4
DEC_BATCH = 128
DEC_SEQ = 1
PAST_LEN = 8192
PAGE_SIZE = 128

N_EVEN = (DEPTH + 1) // 2
N_ODD = DEPTH // 2
EPS = 1e-6

GDN_HEADS = 16
GDN_DK = 128
GDN_DV = 128
GDN_CONV = 4
GDN_CHUNK = 64
GDN_QK_W = GDN_HEADS * GDN_DK
GDN_V_W = GDN_HEADS * GDN_DV
GDN_CONV_W = 2 * GDN_QK_W + GDN_V_W
ML_HEADS = 8
ML_DQK = 128
ML_DV = 256
ML_CHUNK = 64
ML_QK_W = ML_HEADS * ML_DQK
ML_V_W = ML_HEADS * ML_DV
EVEN_SIZES = (GDN_CONV_W, GDN_HEADS, GDN_HEADS, GDN_V_W, ML_QK_W, ML_QK_W, ML_V_W, ML_HEADS, ML_HEADS, ML_V_W)
EVEN_SPLITS = tuple(int(s) for s in np.cumsum(EVEN_SIZES)[:-1])
IN_EVEN = int(sum(EVEN_SIZES))
MIX_W_EVEN = GDN_V_W + ML_V_W
MLA_HEADS = 32
MLA_Q_LORA = 1024
MLA_KV_LORA = 512
MLA_NOPE = 128
MLA_ROPE = 64
MLA_V = 128
MLA_QK = MLA_NOPE + MLA_ROPE
MLA_SCALE = MLA_QK ** -0.5
IN_MLA = MLA_Q_LORA + MLA_KV_LORA + MLA_ROPE
ROPE_THETA = 10000.0
ATTN_BLOCK = 128
D_FF = ((8 * D_MODEL // 3 + 255) // 256) * 256
FFN_CONV = 3

kernel_name = 'hybrid_gdn_mlstm_mla_convffn_step'

F32 = jnp.float32


def rmsnorm(x, g):
    xf = x.astype(F32)
    y = xf * lax.rsqrt(jnp.mean(xf * xf, axis=-1, keepdims=True) + EPS)
    return (y * g.astype(F32)).astype(x.dtype)


def l2norm(x):
    xf = x.astype(F32)
    return (xf * lax.rsqrt(jnp.sum(xf * xf, axis=-1, keepdims=True) + EPS)).astype(x.dtype)


def causal_dwconv(x, buf, w):
    width = w.shape[0]
    T = x.shape[1]
    xp = jnp.concatenate([buf.astype(x.dtype), x], axis=1)
    y = xp[:, 0:T] * w[0]
    for j in range(1, width):
        y = y + xp[:, j:j + T] * w[j]
    return y, xp[:, xp.shape[1] - (width - 1):]


def rope(x, pos):
    half = x.shape[-1] // 2
    inv = ROPE_THETA ** (-jnp.arange(half, dtype=F32) / half)
    ang = pos.astype(F32)[:, None] * inv[None, :]
    cos = jnp.cos(ang)[None, :, None, :]
    sin = jnp.sin(ang)[None, :, None, :]
    xf = x.astype(F32)
    x1, x2 = xf[..., :half], xf[..., half:]
    return jnp.concatenate([x1 * cos - x2 * sin, x2 * cos + x1 * sin], axis=-1).astype(x.dtype)


def chunked_scan(step, carry, seqs, chunk):
    T = seqs[0].shape[1]
    c = chunk if T % chunk == 0 else T
    n = T // c
    xs = tuple(jnp.swapaxes(a.astype(F32).reshape((a.shape[0], n, c) + a.shape[2:]), 0, 1) for a in seqs)
    carry, ys = lax.scan(step, carry, xs)
    ys = jnp.swapaxes(ys, 0, 1)
    return carry, ys.reshape((ys.shape[0], T) + ys.shape[3:])


def gdn_chunk(S, inp):
    q, k, v, g, beta = inp
    c = q.shape[1]
    q, k, v = (jnp.swapaxes(t, 1, 2) for t in (q, k, v))
    G = jnp.cumsum(jnp.swapaxes(g, 1, 2), axis=-1)
    bt = jnp.swapaxes(beta, 1, 2)
    incl = jnp.tril(jnp.ones((c, c), dtype=bool))
    strict = jnp.tril(jnp.ones((c, c), dtype=bool), -1)
    decay = jnp.exp(jnp.where(incl, G[..., :, None] - G[..., None, :], -jnp.inf))
    A = jnp.where(strict, jnp.einsum('bhtd,bhjd->bhtj', k, k) * decay, 0.0) * bt[..., :, None]
    M = A + jnp.eye(c, dtype=A.dtype)
    gam = jnp.exp(G)
    rhs = bt[..., None] * (v - gam[..., None] * jnp.einsum('bhtd,bhdv->bhtv', k, S))
    U = lax.linalg.triangular_solve(M, rhs, left_side=True, lower=True, unit_diagonal=True)
    qk = jnp.where(incl, jnp.einsum('bhtd,bhjd->bhtj', q, k) * decay, 0.0)
    o = gam[..., None] * jnp.einsum('bhtd,bhdv->bhtv', q, S) + jnp.einsum('bhtj,bhjv->bhtv', qk, U)
    G_end = G[..., -1]
    w_end = jnp.exp(G_end[..., None] - G)
    S_new = jnp.exp(G_end)[..., None, None] * S + jnp.einsum('bhjd,bhjv->bhdv', k * w_end[..., None], U)
    return S_new, jnp.swapaxes(o, 1, 2)


def mlstm_chunk(carry, inp):
    Cs, ns, ms = carry
    q, k, v, ig, lf = inp
    q, k, v = (jnp.swapaxes(t, 1, 2) for t in (q, k, v))
    c = q.shape[2]
    F = jnp.cumsum(jnp.swapaxes(lf, 1, 2), axis=-1)
    ih = jnp.swapaxes(ig, 1, 2)
    m = F + jnp.maximum(ms[..., None], lax.cummax(ih - F, axis=2))
    incl = jnp.tril(jnp.ones((c, c), dtype=bool))
    D = jnp.exp(jnp.where(incl, F[..., :, None] - F[..., None, :] + ih[..., None, :] - m[..., :, None], -jnp.inf))
    inter = jnp.exp(ms[..., None] + F - m)
    qk = jnp.einsum('bhtd,bhjd->bhtj', q, k) * D
    num = inter[..., None] * jnp.einsum('bhtd,bhdv->bhtv', q, Cs) + jnp.einsum('bhtj,bhjv->bhtv', qk, v)
    den = inter * jnp.einsum('bhtd,bhd->bht', q, ns) + jnp.sum(qk, axis=-1)
    h = num / jnp.maximum(jnp.abs(den), jnp.exp(-m))[..., None]
    m_end = m[..., -1]
    w_end = jnp.exp(F[..., -1:] - F + ih - m_end[..., None])
    dec = jnp.exp(ms + F[..., -1] - m_end)
    kw = k * w_end[..., None]
    C_new = dec[..., None, None] * Cs + jnp.einsum('bhjd,bhjv->bhdv', kw, v)
    n_new = dec[..., None] * ns + jnp.sum(kw, axis=2)
    return (C_new, n_new, m_end), jnp.swapaxes(h, 1, 2)


def even_mixer(h, e, P, gdn_S, gdn_buf, ml_C, ml_n, ml_m):
    B, T, _ = h.shape
    proj = h @ P['w_in_even'][e]
    qkv, a, b, z, mq, mk, mv, mi, mf, mo = jnp.split(proj, EVEN_SPLITS, axis=-1)
    qkv, gdn_buf_new = causal_dwconv(qkv, gdn_buf, P['gdn_conv_w'][e])
    qkv = jax.nn.silu(qkv)
    gq, gk, gv = jnp.split(qkv, (GDN_QK_W, 2 * GDN_QK_W), axis=-1)
    gq = l2norm(gq.reshape(B, T, GDN_HEADS, GDN_DK)) * (GDN_DK ** -0.5)
    gk = l2norm(gk.reshape(B, T, GDN_HEADS, GDN_DK))
    gv = gv.reshape(B, T, GDN_HEADS, GDN_DV)
    g = -jnp.exp(P['gdn_a_log'][e].astype(F32)) * jax.nn.softplus(a.astype(F32) + P['gdn_dt_bias'][e].astype(F32))
    beta = jax.nn.sigmoid(b.astype(F32))
    gdn_S_new, go = chunked_scan(gdn_chunk, gdn_S.astype(F32), (gq, gk, gv, g, beta), GDN_CHUNK)
    go = rmsnorm(go, P['gdn_norm'][e]) * jax.nn.silu(z.astype(F32)).reshape(B, T, GDN_HEADS, GDN_DV)
    mq = mq.reshape(B, T, ML_HEADS, ML_DQK)
    mk = mk.reshape(B, T, ML_HEADS, ML_DQK) * (ML_DQK ** -0.5)
    mv = mv.reshape(B, T, ML_HEADS, ML_DV)
    ig = mi.astype(F32) + P['mlstm_b_i'][e].astype(F32)
    lf = jax.nn.log_sigmoid(mf.astype(F32) + P['mlstm_b_f'][e].astype(F32))
    carry0 = (ml_C.astype(F32), ml_n.astype(F32), ml_m.astype(F32))
    (C_new, n_new, m_new), mh = chunked_scan(mlstm_chunk, carry0, (mq, mk, mv, ig, lf), ML_CHUNK)
    mh = rmsnorm(mh, P['mlstm_norm'][e]) * jax.nn.sigmoid(mo.astype(F32)).reshape(B, T, ML_HEADS, ML_DV)
    mixed = jnp.concatenate([go.reshape(B, T, GDN_V_W), mh.reshape(B, T, ML_V_W)], axis=-1).astype(h.dtype)
    return mixed @ P['w_out_even'][e], (gdn_S_new, gdn_buf_new, C_new, n_new, m_new)


def mla_prompt_attend(q_nope, q_rope, ckv, kr, w_ukv):
    B, T = q_nope.shape[:2]
    kv = jnp.einsum('bsc,chd->bshd', ckv, w_ukv)
    k_nope, v = kv[..., :MLA_NOPE], kv[..., MLA_NOPE:]
    blk = ATTN_BLOCK if T % ATTN_BLOCK == 0 else T
    nb = T // blk
    kpos = jnp.arange(T)

    def to_blocks(t):
        return jnp.swapaxes(t.reshape((B, nb, blk) + t.shape[2:]), 0, 1)

    def one_block(args):
        i, qn, qr = args
        s = (jnp.einsum('bqhd,bkhd->bhqk', qn, k_nope) + jnp.einsum('bqhr,bkr->bhqk', qr, kr)).astype(F32) * MLA_SCALE
        qpos = i * blk + jnp.arange(blk)
        s = jnp.where(kpos[None, :] <= qpos[:, None], s, -jnp.inf)
        p = jax.nn.softmax(s, axis=-1).astype(v.dtype)
        return jnp.einsum('bhqk,bkhd->bqhd', p, v)

    o = lax.map(one_block, (jnp.arange(nb), to_blocks(q_nope), to_blocks(q_rope)))
    return jnp.swapaxes(o, 0, 1).reshape(B, T, MLA_HEADS, MLA_V)


def mla_sample_attend(q_nope, q_rope, ckv, kr, pool_ckv, pool_kr, li, page_table, w_ukv):
    Bd, T = q_nope.shape[:2]
    ckv_past = pool_ckv[li, page_table].reshape(Bd, -1, MLA_KV_LORA)
    kr_past = pool_kr[li, page_table].reshape(Bd, -1, MLA_ROPE)
    n_past = ckv_past.shape[1]
    w_uk, w_uv = w_ukv[..., :MLA_NOPE], w_ukv[..., MLA_NOPE:]
    q_lat = jnp.einsum('bthd,chd->bthc', q_nope, w_uk)
    s_past = jnp.einsum('bthc,bsc->bhts', q_lat, ckv_past) + jnp.einsum('bthr,bsr->bhts', q_rope, kr_past)
    s_new = jnp.einsum('bthc,bsc->bhts', q_lat, ckv) + jnp.einsum('bthr,bsr->bhts', q_rope, kr)
    s_new = jnp.where(jnp.tril(jnp.ones((T, T), dtype=bool)), s_new.astype(F32), -jnp.inf)
    s = jnp.concatenate([s_past.astype(F32), s_new], axis=-1) * MLA_SCALE
    p = jax.nn.softmax(s, axis=-1).astype(ckv.dtype)
    o_lat = jnp.einsum('bhts,bsc->bthc', p[..., :n_past], ckv_past) + jnp.einsum('bhts,bsc->bthc', p[..., n_past:], ckv)
    return jnp.einsum('bthc,chd->bthd', o_lat, w_uv)


def odd_mixer(h, li, pos, P, attend):
    B, T, _ = h.shape
    proj = h @ P['w_in_mla'][li]
    cq, ckv, kr = jnp.split(proj, (MLA_Q_LORA, MLA_Q_LORA + MLA_KV_LORA), axis=-1)
    q = (rmsnorm(cq, P['mla_q_norm'][li]) @ P['w_uq'][li]).reshape(B, T, MLA_HEADS, MLA_QK)
    q_nope = q[..., :MLA_NOPE]
    q_rope = rope(q[..., MLA_NOPE:], pos)
    ckv = rmsnorm(ckv, P['mla_kv_norm'][li])
    kr = rope(kr[:, :, None, :], pos)[:, :, 0, :]
    att = attend(li, q_nope, q_rope, ckv, kr, P['w_ukv'][li])
    return att.reshape(B, T, MLA_HEADS * MLA_V) @ P['w_o_mla'][li], ckv, kr


def conv_ffn(h, w_up, conv_w, w_down, buf):
    u = h @ w_up
    uc, buf_new = causal_dwconv(u, buf, conv_w)
    gate, val = uc[..., :D_FF], uc[..., D_FF:]
    return (jax.nn.silu(gate) * val) @ w_down, buf_new


def trunk(x, pos, gdn_S, gdn_buf, ml_C, ml_n, ml_m, ffn_buf, attend, P):
    h = x
    ckv_rows, kr_rows, gS, gB, mC, mN, mM, fB = [], [], [], [], [], [], [], []
    for l in range(DEPTH):
        hn = rmsnorm(h, P['norm_mix_pre'][l])
        if l % 2 == 0:
            e = l // 2
            out, (s_g, b_g, c_m, n_m, m_m) = even_mixer(hn, e, P, gdn_S[e], gdn_buf[e], ml_C[e], ml_n[e], ml_m[e])
            gS.append(s_g); gB.append(b_g); mC.append(c_m); mN.append(n_m); mM.append(m_m)
        else:
            out, ckv, kr = odd_mixer(hn, l // 2, pos, P, attend)
            ckv_rows.append(ckv); kr_rows.append(kr)
        h = h + rmsnorm(out, P['norm_mix_post'][l])
        f, fb = conv_ffn(rmsnorm(h, P['norm_ffn_pre'][l]), P['w_up'][l], P['ffn_conv_w'][l], P['w_down'][l], ffn_buf[l])
        fB.append(fb)
        h = h + rmsnorm(f, P['norm_ffn_post'][l])
    dt = x.dtype
    return (h, jnp.stack(ckv_rows).astype(dt), jnp.stack(kr_rows).astype(dt), jnp.stack(gS).astype(dt),
            jnp.stack(gB).astype(dt), jnp.stack(mC).astype(dt), jnp.stack(mN).astype(dt),
            jnp.stack(mM).astype(dt), jnp.stack(fB).astype(dt))


def setup_inputs(seed: int = 0) -> dict:
    key = jax.random.key(seed)
    ks = iter(jax.random.split(key, 48))

    def nrm(shape, scale):
        return jax.random.normal(next(ks), shape, F32) * scale

    def gain(shape):
        return 1.0 + nrm(shape, 0.02)

    n_pages = PAST_LEN // PAGE_SIZE
    used = DEC_BATCH * n_pages
    n_pool = used + max(used // 4, 1)
    page_table = jax.random.permutation(next(ks), n_pool)[:used].reshape(DEC_BATCH, n_pages).astype(jnp.int32)
    dt = jnp.exp(jax.random.uniform(next(ks), (N_EVEN, GDN_HEADS), F32, math.log(1e-3), math.log(1e-1)))
    return {
        'x_prompt': nrm((BATCH, SEQ, D_MODEL), 1.0),
        'x_sample': nrm((DEC_BATCH, DEC_SEQ, D_MODEL), 1.0),
        'cache_mla_ckv': nrm((N_ODD, n_pool, PAGE_SIZE, MLA_KV_LORA), 1.0),
        'cache_mla_krope': nrm((N_ODD, n_pool, PAGE_SIZE, MLA_ROPE), 1.0),
        'page_table': page_table,
        'state_gdn': nrm((N_EVEN, DEC_BATCH, GDN_HEADS, GDN_DK, GDN_DV), 0.1),
        'state_gdn_conv': nrm((N_EVEN, DEC_BATCH, GDN_CONV - 1, GDN_CONV_W), 1.0),
        'state_mlstm_C': nrm((N_EVEN, DEC_BATCH, ML_HEADS, ML_DQK, ML_DV), 0.3),
        'state_mlstm_n': nrm((N_EVEN, DEC_BATCH, ML_HEADS, ML_DQK), 0.3),
        'state_mlstm_m': nrm((N_EVEN, DEC_BATCH, ML_HEADS), 1.0),
        'state_ffn_conv': nrm((DEPTH, DEC_BATCH, FFN_CONV - 1, 2 * D_FF), 1.0),
        'w_in_even': nrm((N_EVEN, D_MODEL, IN_EVEN), D_MODEL ** -0.5),
        'gdn_conv_w': nrm((N_EVEN, GDN_CONV, GDN_CONV_W), GDN_CONV ** -0.5),
        'gdn_a_log': jnp.log(jax.random.uniform(next(ks), (N_EVEN, GDN_HEADS), F32, 1.0, 16.0)),
        'gdn_dt_bias': dt + jnp.log(-jnp.expm1(-dt)),
        'gdn_norm': gain((N_EVEN, GDN_DV)),
        'mlstm_b_i': nrm((N_EVEN, ML_HEADS), 0.1),
        'mlstm_b_f': jax.random.uniform(next(ks), (N_EVEN, ML_HEADS), F32, 3.0, 6.0),
        'mlstm_norm': gain((N_EVEN, ML_HEADS, ML_DV)),
        'w_out_even': nrm((N_EVEN, MIX_W_EVEN, D_MODEL), MIX_W_EVEN ** -0.5),
        'w_in_mla': nrm((N_ODD, D_MODEL, IN_MLA), D_MODEL ** -0.5),
        'mla_q_norm': gain((N_ODD, MLA_Q_LORA)),
        'mla_kv_norm': gain((N_ODD, MLA_KV_LORA)),
        'w_uq': nrm((N_ODD, MLA_Q_LORA, MLA_HEADS * MLA_QK), MLA_Q_LORA ** -0.5),
        'w_ukv': nrm((N_ODD, MLA_KV_LORA, MLA_HEADS, MLA_NOPE + MLA_V), MLA_KV_LORA ** -0.5),
        'w_o_mla': nrm((N_ODD, MLA_HEADS * MLA_V, D_MODEL), (MLA_HEADS * MLA_V) ** -0.5),
        'norm_mix_pre': gain((DEPTH, D_MODEL)),
        'norm_mix_post': gain((DEPTH, D_MODEL)),
        'norm_ffn_pre': gain((DEPTH, D_MODEL)),
        'norm_ffn_post': gain((DEPTH, D_MODEL)),
        'w_up': nrm((DEPTH, D_MODEL, 2 * D_FF), D_MODEL ** -0.5),
        'ffn_conv_w': nrm((DEPTH, FFN_CONV, 2 * D_FF), FFN_CONV ** -0.5),
        'w_down': nrm((DEPTH, D_FF, D_MODEL), D_FF ** -0.5),
    }


def reference(x_prompt, x_sample, cache_mla_ckv, cache_mla_krope, page_table, state_gdn, state_gdn_conv,
              state_mlstm_C, state_mlstm_n, state_mlstm_m, state_ffn_conv, w_in_even, gdn_conv_w, gdn_a_log,
              gdn_dt_bias, gdn_norm, mlstm_b_i, mlstm_b_f, mlstm_norm, w_out_even, w_in_mla, mla_q_norm,
              mla_kv_norm, w_uq, w_ukv, w_o_mla, norm_mix_pre, norm_mix_post, norm_ffn_pre, norm_ffn_post,
              w_up, ffn_conv_w, w_down):
    P = {'w_in_even': w_in_even, 'gdn_conv_w': gdn_conv_w, 'gdn_a_log': gdn_a_log, 'gdn_dt_bias': gdn_dt_bias,
         'gdn_norm': gdn_norm, 'mlstm_b_i': mlstm_b_i, 'mlstm_b_f': mlstm_b_f, 'mlstm_norm': mlstm_norm,
         'w_out_even': w_out_even, 'w_in_mla': w_in_mla, 'mla_q_norm': mla_q_norm, 'mla_kv_norm': mla_kv_norm,
         'w_uq': w_uq, 'w_ukv': w_ukv, 'w_o_mla': w_o_mla, 'norm_mix_pre': norm_mix_pre,
         'norm_mix_post': norm_mix_post, 'norm_ffn_pre': norm_ffn_pre, 'norm_ffn_post': norm_ffn_post,
         'w_up': w_up, 'ffn_conv_w': ffn_conv_w, 'w_down': w_down}
    Bp, Tp, _ = x_prompt.shape
    Bd, Td, _ = x_sample.shape
    past_len = page_table.shape[1] * PAGE_SIZE
    dt = x_prompt.dtype

    pos_p = jnp.arange(Tp)
    (y_prompt, p_ckv, p_krope, p_gdn, p_gdn_conv, p_mC, p_mn, p_mm, p_ffn_conv) = trunk(
        x_prompt, pos_p,
        jnp.zeros((N_EVEN, Bp, GDN_HEADS, GDN_DK, GDN_DV), F32),
        jnp.zeros((N_EVEN, Bp, GDN_CONV - 1, GDN_CONV_W), dt),
        jnp.zeros((N_EVEN, Bp, ML_HEADS, ML_DQK, ML_DV), F32),
        jnp.zeros((N_EVEN, Bp, ML_HEADS, ML_DQK), F32),
        jnp.zeros((N_EVEN, Bp, ML_HEADS), F32),
        jnp.zeros((DEPTH, Bp, FFN_CONV - 1, 2 * D_FF), dt),
        lambda li, qn, qr, ckv, kr, w: mla_prompt_attend(qn, qr, ckv, kr, w),
        P)

    pos_s = past_len + jnp.arange(Td)
    (y_sample, s_ckv, s_krope, s_gdn, s_gdn_conv, s_mC, s_mn, s_mm, s_ffn_conv) = trunk(
        x_sample, pos_s, state_gdn, state_gdn_conv, state_mlstm_C, state_mlstm_n, state_mlstm_m, state_ffn_conv,
        lambda li, qn, qr, ckv, kr, w: mla_sample_attend(qn, qr, ckv, kr, cache_mla_ckv, cache_mla_krope, li, page_table, w),
        P)

    return (y_prompt, y_sample,
            p_ckv, p_krope, p_gdn, p_gdn_conv, p_mC, p_mn, p_mm, p_ffn_conv,
            s_ckv, s_krope, s_gdn, s_gdn_conv, s_mC, s_mn, s_mm, s_ffn_conv)
```

```python
import functools
import math

import jax
import jax.numpy as jnp
import numpy as np
from jax import lax
from jax.experimental import pallas as pl
from jax.experimental.pallas import tpu as pltpu

F32 = jnp.float32
BF16 = jnp.bfloat16

EPS = 1e-6
DEPTH = 4
PAGE_SIZE = 128
GDN_HEADS, GDN_DK, GDN_DV, GDN_CONV, CHUNK = 16, 128, 128, 4, 64
GDN_QK_W = GDN_HEADS * GDN_DK
GDN_V_W = GDN_HEADS * GDN_DV
GDN_CONV_W = 2 * GDN_QK_W + GDN_V_W
ML_HEADS, ML_DQK, ML_DV = 8, 128, 256
ML_QK_W = ML_HEADS * ML_DQK
ML_V_W = ML_HEADS * ML_DV
MLA_HEADS, MLA_Q_LORA, MLA_KV_LORA, MLA_NOPE, MLA_ROPE, MLA_V = 32, 1024, 512, 128, 64, 128
MLA_QK = MLA_NOPE + MLA_ROPE
MLA_SCALE = MLA_QK ** -0.5
ROPE_THETA = 10000.0
FFN_CONV = 3

VMEM_LIMIT_BYTES = 48 * 1024 * 1024
LANES = 128

_NT = (((1,), (1,)), ((), ()))
_TN = (((0,), (0,)), ((), ()))


def _params(*sem):
    return pltpu.CompilerParams(dimension_semantics=sem, vmem_limit_bytes=VMEM_LIMIT_BYTES)


def _pick(n, cands):
    for c in cands:
        if n % c == 0:
            return c
    raise ValueError(f"no tile in {cands} divides {n}")


def _mm_body(x_ref, w_ref, o_ref, acc_ref, *, nk):
    k = pl.program_id(2)
    prod = jnp.dot(x_ref[...].astype(BF16), w_ref[...].astype(BF16), preferred_element_type=F32)

    @pl.when(k == 0)
    def _():
        acc_ref[...] = prod

    @pl.when(k > 0)
    def _():
        acc_ref[...] += prod

    @pl.when(k == nk - 1)
    def _():
        o_ref[...] = acc_ref[...].astype(o_ref.dtype)


def mm(x, w, layer=None, col0=0, ncols=None, out_dtype=F32):
    M, K = x.shape
    N = w.shape[-1] if ncols is None else ncols
    tm = _pick(M, (2048, 1024, 512, 256, 128))
    tn = _pick(math.gcd(N, col0) if col0 else N, (1024, 512, 256, 128))
    tk = _pick(K, (512, 256, 128))
    nk = K // tk
    jb = col0 // tn
    if w.ndim == 3:
        w_spec = pl.BlockSpec((None, tk, tn), lambda i, j, k: (layer, k, j + jb))
    else:
        w_spec = pl.BlockSpec((tk, tn), lambda i, j, k: (k, j + jb))
    return pl.pallas_call(
        functools.partial(_mm_body, nk=nk),
        grid=(M // tm, N // tn, nk),
        in_specs=[pl.BlockSpec((tm, tk), lambda i, j, k: (i, k)), w_spec],
        out_specs=pl.BlockSpec((tm, tn), lambda i, j, k: (i, j)),
        out_shape=jax.ShapeDtypeStruct((M, N), out_dtype),
        scratch_shapes=[pltpu.VMEM((tm, tn), F32)],
        compiler_params=_params("parallel", "parallel", "arbitrary"),
        name="mm",
    )(x, w)


def _head_mm_body(x_ref, w_ref, o_ref, *, dims):
    o_ref[...] = lax.dot_general(x_ref[...].astype(BF16), w_ref[...].astype(BF16), dims,
                                 preferred_element_type=F32)


def head_absorb_q(qn, w_ukv2d):
    Bd = qn.shape[0]
    C = w_ukv2d.shape[0]
    return pl.pallas_call(
        functools.partial(_head_mm_body, dims=_NT),
        grid=(MLA_HEADS,),
        in_specs=[pl.BlockSpec((Bd, MLA_NOPE), lambda h: (0, h)),
                  pl.BlockSpec((C, MLA_NOPE), lambda h: (0, 2 * h))],
        out_specs=pl.BlockSpec((Bd, C), lambda h: (0, h)),
        out_shape=jax.ShapeDtypeStruct((Bd, MLA_HEADS * C), F32),
        compiler_params=_params("parallel"),
        name="mla_absorb_q",
    )(qn, w_ukv2d)


def head_absorb_v(o_lat, w_ukv2d):
    Bd = o_lat.shape[0]
    C = w_ukv2d.shape[0]
    return pl.pallas_call(
        functools.partial(_head_mm_body, dims=(((1,), (0,)), ((), ()))),
        grid=(MLA_HEADS,),
        in_specs=[pl.BlockSpec((Bd, C), lambda h: (0, h)),
                  pl.BlockSpec((C, MLA_V), lambda h: (0, 2 * h + 1))],
        out_specs=pl.BlockSpec((Bd, MLA_V), lambda h: (0, h)),
        out_shape=jax.ShapeDtypeStruct((Bd, MLA_HEADS * MLA_V), F32),
        compiler_params=_params("parallel"),
        name="mla_absorb_v",
    )(o_lat, w_ukv2d)


def _flash_body(qn_ref, qr_ref, kv_ref, kr_ref, o_ref, m_ref, l_ref, acc_ref, *, tq, hpb):
    qi = pl.program_id(2)
    ki = pl.program_id(3)

    @pl.when(ki == 0)
    def _():
        m_ref[...] = jnp.full(m_ref.shape, -jnp.inf, F32)
        l_ref[...] = jnp.zeros(l_ref.shape, F32)
        acc_ref[...] = jnp.zeros(acc_ref.shape, F32)

    @pl.when(ki <= qi)
    def _():
        kr = kr_ref[...].astype(BF16)
        row = lax.broadcasted_iota(jnp.int32, (tq, tq), 0)
        col = lax.broadcasted_iota(jnp.int32, (tq, tq), 1)
        keep = jnp.logical_or(ki < qi, col <= row)
        for h in range(hpb):
            qn = qn_ref[:, h * MLA_NOPE:(h + 1) * MLA_NOPE].astype(BF16)
            qr = qr_ref[:, h * MLA_ROPE:(h + 1) * MLA_ROPE].astype(BF16)
            kn = kv_ref[:, h * 2 * MLA_NOPE:h * 2 * MLA_NOPE + MLA_NOPE].astype(BF16)
            v = kv_ref[:, h * 2 * MLA_NOPE + MLA_NOPE:(h + 1) * 2 * MLA_NOPE].astype(BF16)
            s = (lax.dot_general(qn, kn, _NT, preferred_element_type=F32)
                 + lax.dot_general(qr, kr, _NT, preferred_element_type=F32)) * MLA_SCALE
            s = jnp.where(keep, s, -jnp.inf)
            m_old = m_ref[h]
            m_new = jnp.maximum(m_old, jnp.max(s, axis=1, keepdims=True))
            alpha = jnp.exp(m_old - m_new)
            p = jnp.exp(s - m_new)
            l_ref[h] = alpha * l_ref[h] + jnp.sum(p, axis=1, keepdims=True)
            acc_ref[h] = alpha * acc_ref[h] + jnp.dot(p.astype(BF16), v, preferred_element_type=F32)
            m_ref[h] = m_new

    @pl.when(ki == qi)
    def _():
        for h in range(hpb):
            o_ref[:, h * MLA_V:(h + 1) * MLA_V] = acc_ref[h] / l_ref[h]


def flash_prompt(qn, qr, kv, kr, B, T):
    tq = _pick(T, (512, 256, 128))
    hpb = 2
    nq = T // tq
    grid = (B, MLA_HEADS // hpb, nq, nq)
    return pl.pallas_call(
        functools.partial(_flash_body, tq=tq, hpb=hpb),
        grid=grid,
        in_specs=[
            pl.BlockSpec((tq, hpb * MLA_NOPE), lambda b, h, qi, ki: (b * nq + qi, h)),
            pl.BlockSpec((tq, hpb * MLA_ROPE), lambda b, h, qi, ki: (b * nq + qi, h)),
            pl.BlockSpec((tq, hpb * 2 * MLA_NOPE), lambda b, h, qi, ki: (b * nq + jnp.minimum(ki, qi), h)),
            pl.BlockSpec((tq, MLA_ROPE), lambda b, h, qi, ki: (b * nq + jnp.minimum(ki, qi), 0)),
        ],
        out_specs=pl.BlockSpec((tq, hpb * MLA_V), lambda b, h, qi, ki: (b * nq + qi, h)),
        out_shape=jax.ShapeDtypeStruct((B * T, MLA_HEADS * MLA_V), F32),
        scratch_shapes=[pltpu.VMEM((hpb, tq, 1), F32), pltpu.VMEM((hpb, tq, 1), F32),
                        pltpu.VMEM((hpb, tq, MLA_V), F32)],
        compiler_params=_params("parallel", "parallel", "parallel", "arbitrary"),
        name="mla_flash_prompt",
    )(qn, qr, kv, kr)


def _decode_body(pt_ref, ql_ref, qr_ref, cnew_ref, krnew_ref, *rest, npg, ng):
    ckv_refs = rest[:npg]
    kr_refs = rest[npg:2 * npg]
    o_ref = rest[2 * npg]
    m_ref, l_ref, acc_ref = rest[2 * npg + 1:]
    g = pl.program_id(1)

    @pl.when(g == 0)
    def _():
        m_ref[...] = jnp.full(m_ref.shape, -jnp.inf, F32)
        l_ref[...] = jnp.zeros(l_ref.shape, F32)
        acc_ref[...] = jnp.zeros(acc_ref.shape, F32)

    ql = ql_ref[0].astype(BF16)
    qr = qr_ref[0].astype(BF16)
    s_parts = []
    for i in range(npg):
        c = ckv_refs[i][...].astype(BF16)
        kr = kr_refs[i][...].astype(BF16)
        s_parts.append(lax.dot_general(ql, c, _NT, preferred_element_type=F32)
                       + lax.dot_general(qr, kr, _NT, preferred_element_type=F32))
    s = jnp.concatenate(s_parts, axis=1) * MLA_SCALE
    m_old = m_ref[...]
    m_new = jnp.maximum(m_old, jnp.max(s, axis=1, keepdims=True))
    alpha = jnp.exp(m_old - m_new)
    p = jnp.exp(s - m_new)
    l_new = alpha * l_ref[...] + jnp.sum(p, axis=1, keepdims=True)
    pv = jnp.zeros(acc_ref.shape, F32)
    for i in range(npg):
        c = ckv_refs[i][...].astype(BF16)
        pv = pv + jnp.dot(p[:, i * PAGE_SIZE:(i + 1) * PAGE_SIZE].astype(BF16), c, preferred_element_type=F32)
    acc_new = alpha * acc_ref[...] + pv
    m_ref[...] = m_new
    l_ref[...] = l_new
    acc_ref[...] = acc_new

    @pl.when(g == ng - 1)
    def _():
        cnew = cnew_ref[0]
        krnew = krnew_ref[0]
        s_self = (jnp.sum(ql_ref[0] * cnew, axis=1, keepdims=True)
                  + jnp.sum(qr_ref[0] * krnew, axis=1, keepdims=True)) * MLA_SCALE
        m_f = jnp.maximum(m_new, s_self)
        a_f = jnp.exp(m_new - m_f)
        p_self = jnp.exp(s_self - m_f)
        l_f = a_f * l_new + p_self
        o_ref[0] = (a_f * acc_new + p_self * cnew) / l_f


def decode_attend(q_lat, q_rope, ckv_new, kr_new, pool_ckv, pool_kr, li, page_table):
    Bd, H, C = q_lat.shape
    R = q_rope.shape[-1]
    n_pages = page_table.shape[1]
    npg = _pick(n_pages, (8, 4, 2, 1))
    ng = n_pages // npg

    def page_spec(i, width):
        return pl.BlockSpec((None, None, PAGE_SIZE, width), lambda b, g, pt: (li, pt[b, g * npg + i], 0, 0))

    in_specs = [
        pl.BlockSpec((1, H, C), lambda b, g, pt: (b, 0, 0)),
        pl.BlockSpec((1, H, R), lambda b, g, pt: (b, 0, 0)),
        pl.BlockSpec((1, 1, C), lambda b, g, pt: (b, 0, 0)),
        pl.BlockSpec((1, 1, R), lambda b, g, pt: (b, 0, 0)),
    ] + [page_spec(i, C) for i in range(npg)] + [page_spec(i, R) for i in range(npg)]
    grid_spec = pltpu.PrefetchScalarGridSpec(
        num_scalar_prefetch=1,
        grid=(Bd, ng),
        in_specs=in_specs,
        out_specs=pl.BlockSpec((1, H, C), lambda b, g, pt: (b, 0, 0)),
        scratch_shapes=[pltpu.VMEM((H, 1), F32), pltpu.VMEM((H, 1), F32), pltpu.VMEM((H, C), F32)],
    )
    return pl.pallas_call(
        functools.partial(_decode_body, npg=npg, ng=ng),
        grid_spec=grid_spec,
        out_shape=jax.ShapeDtypeStruct((Bd, H, C), F32),
        compiler_params=_params("parallel", "arbitrary"),
        name="mla_decode",
    )(page_table, q_lat, q_rope, ckv_new, kr_new, *([pool_ckv] * npg), *([pool_kr] * npg))


def _to_col(row, n):
    eye = lax.broadcasted_iota(jnp.int32, (n, n), 0) == lax.broadcasted_iota(jnp.int32, (n, n), 1)
    return jnp.sum(jnp.where(eye, jnp.broadcast_to(row, (n, n)), 0.0), axis=1, keepdims=True)


def _tri(n):
    t = lax.broadcasted_iota(jnp.int32, (n, n), 0)
    j = lax.broadcasted_iota(jnp.int32, (n, n), 1)
    return t, j


def _cumsum_col_row(row, n):
    t, j = _tri(n)
    col = _to_col(row, n)
    c_col = jnp.sum(jnp.where(j <= t, jnp.broadcast_to(row, (n, n)), 0.0), axis=1, keepdims=True)
    c_row = jnp.sum(jnp.where(t <= j, jnp.broadcast_to(col, (n, n)), 0.0), axis=0, keepdims=True)
    return c_col, c_row


def _split_bf16(x):
    hi = x.astype(BF16)
    return hi, (x - hi.astype(F32)).astype(BF16)


def _dot3(a, b):
    ah, al = _split_bf16(a)
    bh, bl = _split_bf16(b)
    return (jnp.dot(ah, bh, preferred_element_type=F32)
            + (jnp.dot(ah, bl, preferred_element_type=F32) + jnp.dot(al, bh, preferred_element_type=F32)))


_INV_BASE = 16


def _unit_lower_inverse(A, t, j, eye, c):
    sh = _INV_BASE.bit_length() - 1
    N = jnp.where((t >> sh) == (j >> sh), -A, 0.0)
    Tm = eye + N
    pw = 1
    while 2 * pw < _INV_BASE:
        N = _dot3(N, N)
        Tm = Tm + _dot3(Tm, N)
        pw *= 2
    while (1 << sh) < c:
        lower_left = ((t >> (sh + 1)) == (j >> (sh + 1))) & (((t >> sh) & 1) == 1) & (((j >> sh) & 1) == 0)
        Tm = Tm - _dot3(_dot3(Tm, jnp.where(lower_left, A, 0.0)), Tm)
        sh += 1
    return Tm


def _gdn_pre_body(q_ref, k_ref, v_ref, g_ref, b_ref, u0_ref, w_ref, qg_ref, kw_ref, p_ref, eg_ref, *, cb):
    c = CHUNK
    t, j = _tri(c)
    incl = j <= t
    strict = j < t
    eye = (t == j).astype(F32)
    for ci in range(cb):
        r = slice(ci * c, (ci + 1) * c)
        q = q_ref[r, :]
        k = k_ref[r, :]
        v = v_ref[r, :]
        g_row = g_ref[0, 0, ci:ci + 1, :]
        beta_col = _to_col(b_ref[0, 0, ci:ci + 1, :], c)
        G_col, G_row = _cumsum_col_row(g_row, c)
        G_end = jnp.sum(g_row, axis=1, keepdims=True)
        decay = jnp.exp(jnp.where(incl, G_col - G_row, -jnp.inf))
        kb = k.astype(BF16)
        kk = lax.dot_general(kb, kb, _NT, preferred_element_type=F32)
        qk = lax.dot_general(q.astype(BF16), kb, _NT, preferred_element_type=F32)
        A = jnp.where(strict, kk * decay, 0.0) * beta_col
        Tm = _unit_lower_inverse(A, t, j, eye, c)
        gam = jnp.exp(G_col)
        rhs = jnp.concatenate([beta_col * v, (beta_col * gam) * k], axis=1)
        sol = jnp.dot(Tm.astype(BF16), rhs.astype(BF16), preferred_element_type=F32)
        u0_ref[r, :] = sol[:, :GDN_DV]
        w_ref[r, :] = sol[:, GDN_DV:].astype(BF16)
        qg_ref[r, :] = (gam * q).astype(BF16)
        w_end = jnp.exp(G_end - G_col)
        kw_ref[r, :] = (k * w_end).astype(BF16)
        p_ref[0, 0, r, :] = jnp.where(incl, qk * decay, 0.0).astype(BF16)
        eg_ref[0, 0, ci:ci + 1, :] = jnp.broadcast_to(jnp.exp(G_end), (1, LANES))


def _gdn_seq_body(u0_ref, w_ref, qg_ref, kw_ref, p_ref, eg_ref, o_ref, s_ref, *, nchunk, hb):
    c = CHUNK
    s_ref[...] = jnp.zeros(s_ref.shape, F32)

    def step(i, carry):
        r = pl.ds(pl.multiple_of(i * c, c), c)
        for h in range(hb):
            cs = slice(h * GDN_DV, (h + 1) * GDN_DV)
            S = s_ref[0, h]
            Sb = S.astype(BF16)
            U = u0_ref[r, cs] - jnp.dot(w_ref[r, cs], Sb, preferred_element_type=F32)
            Ub = U.astype(BF16)
            o_ref[r, cs] = (jnp.dot(qg_ref[r, cs], Sb, preferred_element_type=F32)
                            + jnp.dot(p_ref[0, h, r, :], Ub, preferred_element_type=F32))
            eg = eg_ref[0, h, pl.ds(i, 1), :]
            s_ref[0, h] = eg * S + lax.dot_general(kw_ref[r, cs], Ub, _TN, preferred_element_type=F32)
        return carry

    lax.fori_loop(0, nchunk, step, 0)


def gdn_prompt(q, k, v, g_rows, b_rows, B, T):
    H = GDN_HEADS
    nchunk = T // CHUNK
    cb = 8
    rows = cb * CHUNK
    nb = T // rows
    tok_spec = pl.BlockSpec((rows, GDN_DK), lambda b, h, i: (b * nb + i, h))
    gate_spec = pl.BlockSpec((1, 1, cb, CHUNK), lambda b, h, i: (b, h, i, 0))
    M = B * T
    u0, w, qg, kw, p, eg = pl.pallas_call(
        functools.partial(_gdn_pre_body, cb=cb),
        grid=(B, H, nb),
        in_specs=[tok_spec, tok_spec, tok_spec, gate_spec, gate_spec],
        out_specs=[tok_spec, tok_spec, tok_spec, tok_spec,
                   pl.BlockSpec((1, 1, rows, CHUNK), lambda b, h, i: (b, h, i, 0)),
                   pl.BlockSpec((1, 1, cb, LANES), lambda b, h, i: (b, h, i, 0))],
        out_shape=[jax.ShapeDtypeStruct((M, H * GDN_DV), F32),
                   jax.ShapeDtypeStruct((M, H * GDN_DK), BF16),
                   jax.ShapeDtypeStruct((M, H * GDN_DK), BF16),
                   jax.ShapeDtypeStruct((M, H * GDN_DK), BF16),
                   jax.ShapeDtypeStruct((B, H, T, CHUNK), BF16),
                   jax.ShapeDtypeStruct((B, H, nchunk, LANES), F32)],
        compiler_params=_params("parallel", "parallel", "parallel"),
        name="gdn_chunk_prepare",
    )(q, k, v, g_rows, b_rows)

    hb = 4
    seq_spec = pl.BlockSpec((T, hb * GDN_DV), lambda b, h: (b, h))
    o, S = pl.pallas_call(
        functools.partial(_gdn_seq_body, nchunk=nchunk, hb=hb),
        grid=(B, H // hb),
        in_specs=[seq_spec, seq_spec, seq_spec, seq_spec,
                  pl.BlockSpec((1, hb, T, CHUNK), lambda b, h: (b, h, 0, 0)),
                  pl.BlockSpec((1, hb, nchunk, LANES), lambda b, h: (b, h, 0, 0))],
        out_specs=[seq_spec, pl.BlockSpec((1, hb, GDN_DK, GDN_DV), lambda b, h: (b, h, 0, 0))],
        out_shape=[jax.ShapeDtypeStruct((M, H * GDN_DV), F32),
                   jax.ShapeDtypeStruct((B, H, GDN_DK, GDN_DV), F32)],
        compiler_params=_params("parallel", "parallel"),
        name="gdn_chunk_scan",
    )(u0, w, qg, kw, p, eg)
    return o, S


def _gdn_step_body(q_ref, k_ref, v_ref, eg_ref, b_ref, s_ref, o_ref, so_ref):
    n = GDN_DK
    for h in range(GDN_HEADS):
        S = s_ref[0, h]
        q = q_ref[0, h:h + 1, :]
        k = k_ref[0, h:h + 1, :]
        v = v_ref[0, h:h + 1, :]
        eg = eg_ref[0, h:h + 1, :]
        beta = b_ref[0, h:h + 1, :]
        k_col = _to_col(k, n)
        q_col = _to_col(q, n)
        kS = jnp.sum(k_col * S, axis=0, keepdims=True)
        qS = jnp.sum(q_col * S, axis=0, keepdims=True)
        U = beta * (v - eg * kS)
        qk = jnp.sum(q * k, axis=1, keepdims=True)
        o_ref[0, h:h + 1, :] = eg * qS + qk * U
        so_ref[0, h] = eg * S + k_col * U


def gdn_step(q, k, v, eg, beta, S):
    Bd = q.shape[0]
    vec = pl.BlockSpec((1, GDN_HEADS, GDN_DK), lambda b: (b, 0, 0))
    st = pl.BlockSpec((1, GDN_HEADS, GDN_DK, GDN_DV), lambda b: (b, 0, 0, 0))
    return pl.pallas_call(
        _gdn_step_body,
        grid=(Bd,),
        in_specs=[vec, vec, vec, vec, vec, st],
        out_specs=[vec, st],
        out_shape=[jax.ShapeDtypeStruct((Bd, GDN_HEADS, GDN_DV), F32),
                   jax.ShapeDtypeStruct(S.shape, F32)],
        compiler_params=_params("parallel"),
        name="gdn_step",
    )(q, k, v, eg, beta, S)


def _mlstm_chunk_body(q_ref, k_ref, v_ref, ig_ref, lf_ref, h_ref, c_out, n_out, m_out,
                      c_ref, n_ref, m_ref, *, nchunk):
    c = CHUNK
    c_ref[...] = jnp.zeros(c_ref.shape, F32)
    n_ref[...] = jnp.zeros(n_ref.shape, F32)
    m_ref[...] = jnp.zeros(m_ref.shape, F32)
    t, j = _tri(c)
    incl = j <= t

    def step(i, carry):
        r = pl.ds(pl.multiple_of(i * c, c), c)
        q = q_ref[r, :]
        k = k_ref[r, :]
        v = v_ref[r, :]
        ig_row = ig_ref[0, 0, pl.ds(i, 1), :]
        lf_row = lf_ref[0, 0, pl.ds(i, 1), :]
        ms = m_ref[...]
        F_col, F_row = _cumsum_col_row(lf_row, c)
        F_end = jnp.sum(lf_row, axis=1, keepdims=True)
        r_row = ig_row - F_row
        cm_col = jnp.max(jnp.where(incl, jnp.broadcast_to(r_row, (c, c)), -jnp.inf), axis=1, keepdims=True)
        c_col = jnp.maximum(ms, cm_col)
        m_col = F_col + c_col
        D = jnp.exp(jnp.where(incl, r_row - c_col, -jnp.inf))
        inter = jnp.exp(ms - c_col)
        qb = q.astype(BF16)
        kb = k.astype(BF16)
        vb = v.astype(BF16)
        qk = lax.dot_general(qb, kb, _NT, preferred_element_type=F32) * D
        Cs = c_ref[...]
        num = inter * jnp.dot(qb, Cs.astype(BF16), preferred_element_type=F32) \
            + jnp.dot(qk.astype(BF16), vb, preferred_element_type=F32)
        qn = jnp.sum(q * n_ref[...], axis=1, keepdims=True)
        den = inter * qn + jnp.sum(qk, axis=1, keepdims=True)
        h_ref[r, :] = num / jnp.maximum(jnp.abs(den), jnp.exp(-m_col))
        m_end = F_end + jnp.maximum(ms, jnp.max(r_row, axis=1, keepdims=True))
        w_end = jnp.exp(F_end - F_row + ig_row - m_end)
        dec = jnp.exp(ms + F_end - m_end)
        kw = k * _to_col(w_end, c)
        c_ref[...] = dec * Cs + lax.dot_general(kw.astype(BF16), vb, _TN, preferred_element_type=F32)
        n_ref[...] = dec * n_ref[...] + jnp.sum(kw, axis=0, keepdims=True)
        m_ref[...] = m_end
        return carry

    lax.fori_loop(0, nchunk, step, 0)
    c_out[0, 0] = c_ref[...]
    n_out[0, 0] = n_ref[...]
    m_out[0, 0] = jnp.broadcast_to(m_ref[...], (1, LANES))


def mlstm_prompt(q, k, v, ig_rows, lf_rows, B, T):
    H = ML_HEADS
    nchunk = T // CHUNK
    M = B * T
    qk_spec = pl.BlockSpec((T, ML_DQK), lambda b, h: (b, h))
    v_spec = pl.BlockSpec((T, ML_DV), lambda b, h: (b, h))
    gate_spec = pl.BlockSpec((1, 1, nchunk, CHUNK), lambda b, h: (b, h, 0, 0))
    return pl.pallas_call(
        functools.partial(_mlstm_chunk_body, nchunk=nchunk),
        grid=(B, H),
        in_specs=[qk_spec, qk_spec, v_spec, gate_spec, gate_spec],
        out_specs=[v_spec,
                   pl.BlockSpec((1, 1, ML_DQK, ML_DV), lambda b, h: (b, h, 0, 0)),
                   pl.BlockSpec((1, 1, 1, ML_DQK), lambda b, h: (b, h, 0, 0)),
                   pl.BlockSpec((1, 1, 1, LANES), lambda b, h: (b, h, 0, 0))],
        out_shape=[jax.ShapeDtypeStruct((M, H * ML_DV), F32),
                   jax.ShapeDtypeStruct((B, H, ML_DQK, ML_DV), F32),
                   jax.ShapeDtypeStruct((B, H, 1, ML_DQK), F32),
                   jax.ShapeDtypeStruct((B, H, 1, LANES), F32)],
        scratch_shapes=[pltpu.VMEM((ML_DQK, ML_DV), F32), pltpu.VMEM((1, ML_DQK), F32), pltpu.VMEM((1, 1), F32)],
        compiler_params=_params("parallel", "parallel"),
        name="mlstm_chunk_scan",
    )(q, k, v, ig_rows, lf_rows)


def _mlstm_step_body(q_ref, k_ref, v_ref, ig_ref, lf_ref, ms_ref, c_ref, n_ref, h_ref, co_ref, no_ref, mo_ref):
    nq = ML_DQK
    for h in range(ML_HEADS):
        Cs = c_ref[0, h]
        q = q_ref[0, h:h + 1, :]
        k = k_ref[0, h:h + 1, :]
        v = v_ref[0, h:h + 1, :]
        ns = n_ref[0, h:h + 1, :]
        ig = ig_ref[0, h:h + 1, :][:, :1]
        lf = lf_ref[0, h:h + 1, :][:, :1]
        ms = ms_ref[0, h:h + 1, :][:, :1]
        m = jnp.maximum(ms + lf, ig)
        D = jnp.exp(ig - m)
        inter = jnp.exp(ms + lf - m)
        qk = jnp.sum(q * k, axis=1, keepdims=True) * D
        q_col = _to_col(q, nq)
        kw = k * D
        kw_col = _to_col(kw, nq)
        num = inter * jnp.sum(q_col * Cs, axis=0, keepdims=True) + qk * v
        den = inter * jnp.sum(q * ns, axis=1, keepdims=True) + qk
        h_ref[0, h:h + 1, :] = num / jnp.maximum(jnp.abs(den), jnp.exp(-m))
        co_ref[0, h] = inter * Cs + kw_col * v
        no_ref[0, h:h + 1, :] = inter * ns + kw
        mo_ref[0, h:h + 1, :] = jnp.broadcast_to(m, (1, LANES))


def mlstm_step(q, k, v, ig, lf, ms, C, n):
    Bd = q.shape[0]
    qs = pl.BlockSpec((1, ML_HEADS, ML_DQK), lambda b: (b, 0, 0))
    vs = pl.BlockSpec((1, ML_HEADS, ML_DV), lambda b: (b, 0, 0))
    gs = pl.BlockSpec((1, ML_HEADS, LANES), lambda b: (b, 0, 0))
    cs = pl.BlockSpec((1, ML_HEADS, ML_DQK, ML_DV), lambda b: (b, 0, 0, 0))
    return pl.pallas_call(
        _mlstm_step_body,
        grid=(Bd,),
        in_specs=[qs, qs, vs, gs, gs, gs, cs, qs],
        out_specs=[vs, cs, qs, gs],
        out_shape=[jax.ShapeDtypeStruct((Bd, ML_HEADS, ML_DV), F32),
                   jax.ShapeDtypeStruct(C.shape, F32),
                   jax.ShapeDtypeStruct((Bd, ML_HEADS, ML_DQK), F32),
                   jax.ShapeDtypeStruct((Bd, ML_HEADS, LANES), F32)],
        compiler_params=_params("parallel"),
        name="mlstm_step",
    )(q, k, v, ig, lf, ms, C, n)


def _rmsnorm(x, g):
    xf = x.astype(F32)
    return xf * lax.rsqrt(jnp.mean(xf * xf, axis=-1, keepdims=True) + EPS) * g.astype(F32)


def _l2norm(x):
    return x * lax.rsqrt(jnp.sum(x * x, axis=-1, keepdims=True) + EPS)


def _causal_dwconv(x, buf, w):
    width = w.shape[0]
    T = x.shape[1]
    xp = jnp.concatenate([buf.astype(x.dtype), x], axis=1)
    y = xp[:, 0:T] * w[0]
    for j in range(1, width):
        y = y + xp[:, j:j + T] * w[j]
    return y, xp[:, xp.shape[1] - (width - 1):]


def _rope(x, pos):
    half = x.shape[-1] // 2
    inv = ROPE_THETA ** (-jnp.arange(half, dtype=F32) / half)
    ang = pos.astype(F32)[:, None] * inv[None, :]
    cos = jnp.cos(ang)[None, :, None, :]
    sin = jnp.sin(ang)[None, :, None, :]
    x1, x2 = x[..., :half], x[..., half:]
    return jnp.concatenate([x1 * cos - x2 * sin, x2 * cos + x1 * sin], axis=-1)


_OFF_A = GDN_CONV_W
_OFF_Z = _OFF_A + 2 * GDN_HEADS
_OFF_MI = _OFF_Z + GDN_V_W + 2 * ML_QK_W + ML_V_W
_OFF_MO = _OFF_MI + 2 * ML_HEADS
_MAIN_SIZES = (GDN_V_W, ML_QK_W, ML_QK_W, ML_V_W, ML_V_W)


def _pack_rest(w_e):
    main = jnp.concatenate([w_e[:, _OFF_Z:_OFF_MI], w_e[:, _OFF_MO:]], axis=1).astype(BF16)
    small = jnp.concatenate([w_e[:, _OFF_A:_OFF_Z], w_e[:, _OFF_MI:_OFF_MO]], axis=1)
    small = jnp.pad(small, ((0, 0), (0, LANES - small.shape[1]))).astype(BF16)
    return main, small


def _split_cols(x, sizes):
    o = 0
    outs = []
    for s in sizes:
        outs.append(x[:, o:o + s])
        o += s
    return outs


def _chunk_rows(x, B, T, H):
    return x.reshape(B, T, H).transpose(0, 2, 1).reshape(B, H, T // CHUNK, CHUNK)


def _even_mixer(hn, e, P, W, B, T, state):
    M = B * T
    qkv = mm(hn, P['w_in_even'], layer=e, col0=0, ncols=GDN_CONV_W)
    w_main, w_small = W['w_rest'][e]
    z, mq, mk, mv, mo = _split_cols(mm(hn, w_main), _MAIN_SIZES)
    a, b, mi, mf = _split_cols(mm(hn, w_small), (GDN_HEADS, GDN_HEADS, ML_HEADS, ML_HEADS))
    gdn_buf = jnp.zeros((B, GDN_CONV - 1, GDN_CONV_W), F32) if state is None else state[1]
    qkv, gdn_buf_new = _causal_dwconv(qkv.reshape(B, T, GDN_CONV_W), gdn_buf, P['gdn_conv_w'][e])
    qkv = jax.nn.silu(qkv).reshape(M, GDN_CONV_W)
    gq = (_l2norm(qkv[:, :GDN_QK_W].reshape(M, GDN_HEADS, GDN_DK)) * (GDN_DK ** -0.5))
    gk = _l2norm(qkv[:, GDN_QK_W:2 * GDN_QK_W].reshape(M, GDN_HEADS, GDN_DK))
    gv = qkv[:, 2 * GDN_QK_W:]
    g = -jnp.exp(P['gdn_a_log'][e]) * jax.nn.softplus(a + P['gdn_dt_bias'][e])
    beta = jax.nn.sigmoid(b)
    mk = mk * (ML_DQK ** -0.5)
    ig = mi + P['mlstm_b_i'][e]
    lf = jax.nn.log_sigmoid(mf + P['mlstm_b_f'][e])
    if state is None:
        go, S_new = gdn_prompt(gq.reshape(M, GDN_QK_W), gk.reshape(M, GDN_QK_W), gv,
                               _chunk_rows(g, B, T, GDN_HEADS), _chunk_rows(beta, B, T, GDN_HEADS), B, T)
        mh, C_new, n_new, m_new = mlstm_prompt(mq, mk, mv, _chunk_rows(ig, B, T, ML_HEADS),
                                               _chunk_rows(lf, B, T, ML_HEADS), B, T)
        n_new = n_new[:, :, 0, :]
        m_new = m_new[:, :, 0, 0]
    else:
        bc = lambda x, H: jnp.broadcast_to(x.reshape(M, H, 1), (M, H, LANES))
        go, S_new = gdn_step(gq, gk, gv.reshape(M, GDN_HEADS, GDN_DV), bc(jnp.exp(g), GDN_HEADS),
                             bc(beta, GDN_HEADS), state[0])
        mh, C_new, n_new, m_new = mlstm_step(mq.reshape(M, ML_HEADS, ML_DQK), mk.reshape(M, ML_HEADS, ML_DQK),
                                             mv.reshape(M, ML_HEADS, ML_DV), bc(ig, ML_HEADS), bc(lf, ML_HEADS),
                                             bc(state[4], ML_HEADS), state[2], state[3])
        m_new = m_new[:, :, 0]
    go = _rmsnorm(go.reshape(M, GDN_HEADS, GDN_DV), P['gdn_norm'][e]) * jax.nn.silu(z).reshape(M, GDN_HEADS, GDN_DV)
    mh = _rmsnorm(mh.reshape(M, ML_HEADS, ML_DV), P['mlstm_norm'][e]) * jax.nn.sigmoid(mo).reshape(M, ML_HEADS, ML_DV)
    mixed = jnp.concatenate([go.reshape(M, GDN_V_W), mh.reshape(M, ML_V_W)], axis=-1).astype(BF16)
    out = mm(mixed, P['w_out_even'], layer=e)
    return out, (S_new, gdn_buf_new, C_new, n_new, m_new)


def _odd_mixer(hn, li, pos, P, W, B, T, cache):
    M = B * T
    proj = mm(hn, W['w_in_mla_pad'][li])
    cq = proj[:, :MLA_Q_LORA]
    ckv = proj[:, MLA_Q_LORA:MLA_Q_LORA + MLA_KV_LORA]
    kr = proj[:, MLA_Q_LORA + MLA_KV_LORA:MLA_Q_LORA + MLA_KV_LORA + MLA_ROPE]
    q = mm(_rmsnorm(cq, P['mla_q_norm'][li]).astype(BF16), W['w_uq_perm'][li])
    qn = q[:, :MLA_HEADS * MLA_NOPE]
    qr = _rope(q[:, MLA_HEADS * MLA_NOPE:].reshape(B, T, MLA_HEADS, MLA_ROPE), pos)
    ckv = _rmsnorm(ckv, P['mla_kv_norm'][li])
    kr = _rope(kr.reshape(B, T, 1, MLA_ROPE), pos).reshape(M, MLA_ROPE)
    w_ukv2d = P['w_ukv'][li].reshape(MLA_KV_LORA, MLA_HEADS * (MLA_NOPE + MLA_V))
    if cache is None:
        kv = mm(ckv.astype(BF16), w_ukv2d)
        att = flash_prompt(qn, qr.reshape(M, MLA_HEADS * MLA_ROPE), kv, kr, B, T)
    else:
        q_lat = head_absorb_q(qn, w_ukv2d)
        o_lat = decode_attend(q_lat.reshape(M, MLA_HEADS, MLA_KV_LORA), qr.reshape(M, MLA_HEADS, MLA_ROPE),
                              ckv.reshape(M, 1, MLA_KV_LORA), kr.reshape(M, 1, MLA_ROPE),
                              cache[0], cache[1], li, cache[2])
        att = head_absorb_v(o_lat.reshape(M, MLA_HEADS * MLA_KV_LORA), w_ukv2d)
    out = mm(att.astype(BF16), P['w_o_mla'], layer=li)
    return out, ckv.reshape(B, T, MLA_KV_LORA), kr.reshape(B, T, MLA_ROPE)


def _conv_ffn(hn, l, P, B, T, buf):
    d_ff = P['w_down'].shape[1]
    u = mm(hn, P['w_up'], layer=l)
    uc, buf_new = _causal_dwconv(u.reshape(B, T, 2 * d_ff), buf, P['ffn_conv_w'][l])
    act = (jax.nn.silu(uc[..., :d_ff]) * uc[..., d_ff:]).reshape(B * T, d_ff).astype(BF16)
    return mm(act, P['w_down'], layer=l), buf_new


def _trunk(x, pos, P, W, states, cache):
    B, T, D = x.shape
    M = B * T
    h = x.reshape(M, D)
    ckv_rows, kr_rows, gS, gB, mC, mN, mM, fB = [], [], [], [], [], [], [], []
    for l in range(DEPTH):
        hn = _rmsnorm(h, P['norm_mix_pre'][l]).astype(BF16)
        if l % 2 == 0:
            e = l // 2
            st = None if states is None else tuple(s[e] for s in states[:5])
            out, (s_g, b_g, c_m, n_m, m_m) = _even_mixer(hn, e, P, W, B, T, st)
            gS.append(s_g); gB.append(b_g); mC.append(c_m); mN.append(n_m); mM.append(m_m)
        else:
            out, ckv, kr = _odd_mixer(hn, l // 2, pos, P, W, B, T, cache)
            ckv_rows.append(ckv); kr_rows.append(kr)
        h = h + _rmsnorm(out, P['norm_mix_post'][l])
        fbuf = jnp.zeros((B, FFN_CONV - 1, P['w_up'].shape[-1]), F32) if states is None else states[5][l]
        f, fb = _conv_ffn(_rmsnorm(h, P['norm_ffn_pre'][l]).astype(BF16), l, P, B, T, fbuf)
        fB.append(fb)
        h = h + _rmsnorm(f, P['norm_ffn_post'][l])
    return (h.reshape(B, T, D), jnp.stack(ckv_rows), jnp.stack(kr_rows), jnp.stack(gS), jnp.stack(gB),
            jnp.stack(mC), jnp.stack(mN), jnp.stack(mM), jnp.stack(fB))


def kernel(x_prompt, x_sample, cache_mla_ckv, cache_mla_krope, page_table, state_gdn, state_gdn_conv,
           state_mlstm_C, state_mlstm_n, state_mlstm_m, state_ffn_conv, w_in_even, gdn_conv_w, gdn_a_log,
           gdn_dt_bias, gdn_norm, mlstm_b_i, mlstm_b_f, mlstm_norm, w_out_even, w_in_mla, mla_q_norm,
           mla_kv_norm, w_uq, w_ukv, w_o_mla, norm_mix_pre, norm_mix_post, norm_ffn_pre, norm_ffn_post,
           w_up, ffn_conv_w, w_down):
    P = {'w_in_even': w_in_even, 'gdn_conv_w': gdn_conv_w, 'gdn_a_log': gdn_a_log, 'gdn_dt_bias': gdn_dt_bias,
         'gdn_norm': gdn_norm, 'mlstm_b_i': mlstm_b_i, 'mlstm_b_f': mlstm_b_f, 'mlstm_norm': mlstm_norm,
         'w_out_even': w_out_even, 'w_in_mla': w_in_mla, 'mla_q_norm': mla_q_norm, 'mla_kv_norm': mla_kv_norm,
         'w_uq': w_uq, 'w_ukv': w_ukv, 'w_o_mla': w_o_mla, 'norm_mix_pre': norm_mix_pre,
         'norm_mix_post': norm_mix_post, 'norm_ffn_pre': norm_ffn_pre, 'norm_ffn_post': norm_ffn_post,
         'w_up': w_up, 'ffn_conv_w': ffn_conv_w, 'w_down': w_down}
    n_even, n_odd = w_in_even.shape[0], w_in_mla.shape[0]
    W = {
        'w_rest': [_pack_rest(w_in_even[e]) for e in range(n_even)],
        'w_in_mla_pad': [jnp.pad(w_in_mla[i], ((0, 0), (0, (-w_in_mla.shape[-1]) % 1024))).astype(BF16)
                         for i in range(n_odd)],
        'w_uq_perm': [jnp.concatenate(
            [w_uq[i].reshape(MLA_Q_LORA, MLA_HEADS, MLA_QK)[:, :, :MLA_NOPE].reshape(MLA_Q_LORA, -1),
             w_uq[i].reshape(MLA_Q_LORA, MLA_HEADS, MLA_QK)[:, :, MLA_NOPE:].reshape(MLA_Q_LORA, -1)],
            axis=1).astype(BF16) for i in range(n_odd)],
    }
    Bp, Tp, _ = x_prompt.shape
    Bd, Td, _ = x_sample.shape
    past_len = page_table.shape[1] * PAGE_SIZE

    outs_p = _trunk(x_prompt, jnp.arange(Tp), P, W, None, None)
    states = (state_gdn, state_gdn_conv, state_mlstm_C, state_mlstm_n, state_mlstm_m, state_ffn_conv)
    outs_s = _trunk(x_sample, past_len + jnp.arange(Td), P, W, states,
                    (cache_mla_ckv, cache_mla_krope, page_table))
    return (outs_p[0], outs_s[0]) + tuple(outs_p[1:]) + tuple(outs_s[1:])
```
